```python
import jax, jax.numpy as jnp
from jax import lax
import numpy as np

D_MODEL = 2048
BATCH = 4
SEQ = 2048
DEPTH = 2
DEC_BATCH = 128
DEC_SEQ = 4
PAST_LEN = 2048
PAGE_SIZE = 128

N_META = 16
D_MIX = D_MODEL
N_HEADS_A = 8
HEAD_DIM = 128
C_A = N_HEADS_A * HEAD_DIM
C_B = D_MIX - C_A
D_IN = 3 * C_A + 2 * C_B
CONV_W = 31
Q_BLOCK = 128
N_GROUPS = 4
EXPERTS_PER_GROUP = 4
N_EXPERTS = N_GROUPS * EXPERTS_PER_GROUP
TOP_K = 2
D_EXPERT = 512
EPS = 1e-6
SB_SCALE = HEAD_DIM ** -0.5
SB_BIAS_INIT = -7.0

kernel_name = 'hymba_stickbreak_conformer_hmoe_step'


def _rmsnorm(x, g):
    xf = x.astype(jnp.float32)
    y = xf * lax.rsqrt(jnp.mean(xf * xf, axis=-1, keepdims=True) + EPS)
    return (y * g.astype(jnp.float32)).astype(x.dtype)


def _layernorm(x, g, b):
    xf = x.astype(jnp.float32)
    mu = jnp.mean(xf, axis=-1, keepdims=True)
    xc = xf - mu
    var = jnp.mean(xc * xc, axis=-1, keepdims=True)
    return (xc * lax.rsqrt(var + EPS) * g.astype(jnp.float32) + b.astype(jnp.float32)).astype(x.dtype)


def _in_proj(x, g_attn, w_in, g_q, g_k):
    h = _rmsnorm(x, g_attn)
    p = jnp.einsum('btd,de->bte', h, w_in)
    q, k, v, ua, ub = jnp.split(p, [C_A, 2 * C_A, 3 * C_A, 3 * C_A + C_B], axis=-1)
    shp = x.shape[:2] + (N_HEADS_A, HEAD_DIM)
    q = _rmsnorm(q.reshape(shp), g_q)
    k = _rmsnorm(k.reshape(shp), g_k)
    v = v.reshape(shp)
    u = ua * jax.nn.sigmoid(ub)
    return q, k, v, u


def _sb_weights(z, mask, bias):
    z = z.astype(jnp.float32) * SB_SCALE + bias.astype(jnp.float32)[None, :, None, None]
    log_keep = jnp.where(mask, jax.nn.log_sigmoid(-z), 0.0)
    after = lax.cumsum(log_keep, axis=3, reverse=True) - log_keep
    return jnp.where(mask, jnp.exp(jax.nn.log_sigmoid(z) + after), 0.0)


def _sb_attend(q, k, v, q_pos, k_pos, bias):
    z = jnp.einsum('bqhd,bkhd->bhqk', q, k)
    w = _sb_weights(z, k_pos[None, :] < q_pos[:, None], bias)
    return jnp.einsum('bhqk,bkhd->bqhd', w.astype(v.dtype), v)


def _sb_prompt(q, k, v, bias):
    b, t = q.shape[:2]
    meta_pos = jnp.arange(N_META)
    o_meta = _sb_attend(q[:, :N_META], k[:, :N_META], v[:, :N_META], meta_pos, meta_pos, bias)
    nb = (t - N_META) // Q_BLOCK
    qb = jnp.moveaxis(q[:, N_META:].reshape(b, nb, Q_BLOCK, N_HEADS_A, HEAD_DIM), 1, 0)
    pos = (N_META + jnp.arange(nb * Q_BLOCK)).reshape(nb, Q_BLOCK)
    k_pos = jnp.arange(t)
    ob = lax.map(lambda a: _sb_attend(a[0], k, v, a[1], k_pos, bias), (qb, pos))
    o_real = jnp.moveaxis(ob, 0, 1).reshape(b, nb * Q_BLOCK, C_A)
    return jnp.concatenate([o_meta.reshape(b, N_META, C_A), o_real], axis=1)


def _sb_sample(q, k_new, v_new, k_past, v_past, bias):
    b, s = q.shape[:2]
    p = k_past.shape[1]
    z = jnp.concatenate([jnp.einsum('bqhd,bkhd->bhqk', q, k_past),
                         jnp.einsum('bqhd,bkhd->bhqk', q, k_new)], axis=-1)
    q_pos = p + jnp.arange(s)
    k_pos = jnp.arange(p + s)
    w = _sb_weights(z, k_pos[None, :] < q_pos[:, None], bias).astype(v_new.dtype)
    o = (jnp.einsum('bhqk,bkhd->bqhd', w[..., :p], v_past)
         + jnp.einsum('bhqk,bkhd->bqhd', w[..., p:], v_new))
    return o.reshape(b, s, C_A)


def _conv_module(u_ext, w_dw, b_dw, g_ln, b_ln):
    y = lax.conv_general_dilated(u_ext, w_dw[:, None, :], window_strides=(1,), padding='VALID',
                                 dimension_numbers=('NWC', 'WIO', 'NWC'), feature_group_count=C_B)
    y = _layernorm(y + b_dw, g_ln, b_ln)
    return jax.nn.silu(y)


def _out_proj(a, c, g_oa, g_ob, w_out):
    m = jnp.concatenate([_rmsnorm(a, g_oa), _rmsnorm(c, g_ob)], axis=-1)
    return jnp.einsum('btc,cd->btd', m, w_out)


def _hier_moe(x, g_ffn, w_gr, b_gr, w_er, b_er, w_gate, w_up, w_down):
    bsz, t, d = x.shape
    h = _rmsnorm(x, g_ffn).reshape(bsz * t, d)
    gl = jnp.einsum('nd,dg->ng', h, w_gr).astype(jnp.float32) + b_gr.astype(jnp.float32)
    g_sel = jnp.argmax(gl, axis=-1)
    g_oh = jax.nn.one_hot(g_sel, N_GROUPS, dtype=jnp.float32)
    p_g = jnp.sum(jax.nn.softmax(gl, axis=-1) * g_oh, axis=-1, keepdims=True)
    el = (jnp.einsum('nd,de->ne', h, w_er).astype(jnp.float32) + b_er.astype(jnp.float32))
    el = el.reshape(-1, N_GROUPS, EXPERTS_PER_GROUP)
    el_sel = jnp.einsum('ng,nge->ne', g_oh, el)
    top_v, top_i = lax.top_k(el_sel, TOP_K)
    w_k = jax.nn.softmax(top_v, axis=-1) * p_g
    e_idx = g_sel[:, None] * EXPERTS_PER_GROUP + top_i
    gates = jnp.einsum('nke,nk->ne', jax.nn.one_hot(e_idx, N_EXPERTS, dtype=jnp.float32), w_k)
    hg = jnp.einsum('nd,edf->nef', h, w_gate)
    hu = jnp.einsum('nd,edf->nef', h, w_up)
    act = jax.nn.silu(hg) * hu * gates.astype(h.dtype)[:, :, None]
    y = jnp.einsum('nef,efd->nd', act, w_down)
    return y.reshape(bsz, t, d)


def setup_inputs(seed: int = 0) -> dict:
    key = jax.random.key(seed)
    ks = jax.random.split(key, 32)
    f32 = jnp.float32
    n_pages = PAST_LEN // PAGE_SIZE
    n_used = DEC_BATCH * n_pages
    n_pool = n_used + max(1, n_used // 4)
    nrm = lambda k, shp, s: jax.random.normal(k, shp, f32) * s
    page_table = jax.random.permutation(ks[5], n_pool)[:n_used].reshape(DEC_BATCH, n_pages).astype(jnp.int32)
    return {
        'x_prompt': nrm(ks[0], (BATCH, SEQ, D_MODEL), 1.0),
        'x_sample': nrm(ks[1], (DEC_BATCH, DEC_SEQ, D_MODEL), 1.0),
        'cache_k': nrm(ks[2], (DEPTH, n_pool, PAGE_SIZE, N_HEADS_A, HEAD_DIM), 1.0),
        'cache_v': nrm(ks[3], (DEPTH, n_pool, PAGE_SIZE, N_HEADS_A, HEAD_DIM), 1.0),
        'state_conv': nrm(ks[4], (DEPTH, DEC_BATCH, CONV_W - 1, C_B), 0.5),
        'page_table': page_table,
        'meta_tokens': nrm(ks[6], (N_META, D_MODEL), 1.0),
        'g_attn': 1.0 + nrm(ks[7], (DEPTH, D_MODEL), 0.02),
        'w_in': nrm(ks[8], (DEPTH, D_MODEL, D_IN), D_MODEL ** -0.5),
        'g_q': 1.0 + nrm(ks[9], (DEPTH, HEAD_DIM), 0.02),
        'g_k': 1.0 + nrm(ks[10], (DEPTH, HEAD_DIM), 0.02),
        'sb_bias': SB_BIAS_INIT + nrm(ks[26], (DEPTH, N_HEADS_A), 0.5),
        'w_dw': nrm(ks[11], (DEPTH, CONV_W, C_B), CONV_W ** -0.5),
        'b_dw': nrm(ks[12], (DEPTH, C_B), 0.02),
        'g_conv_ln': 1.0 + nrm(ks[13], (DEPTH, C_B), 0.02),
        'b_conv_ln': nrm(ks[14], (DEPTH, C_B), 0.02),
        'g_out_a': 1.0 + nrm(ks[15], (DEPTH, C_A), 0.02),
        'g_out_b': 1.0 + nrm(ks[16], (DEPTH, C_B), 0.02),
        'w_out': nrm(ks[17], (DEPTH, D_MIX, D_MODEL), D_MIX ** -0.5),
        'g_ffn': 1.0 + nrm(ks[18], (DEPTH, D_MODEL), 0.02),
        'w_group_router': nrm(ks[19], (DEPTH, D_MODEL, N_GROUPS), D_MODEL ** -0.5),
        'b_group_router': nrm(ks[20], (DEPTH, N_GROUPS), 0.01),
        'w_expert_router': nrm(ks[21], (DEPTH, D_MODEL, N_EXPERTS), D_MODEL ** -0.5),
        'b_expert_router': nrm(ks[22], (DEPTH, N_EXPERTS), 0.01),
        'w_gate': nrm(ks[23], (DEPTH, N_EXPERTS, D_MODEL, D_EXPERT), D_MODEL ** -0.5),
        'w_up': nrm(ks[24], (DEPTH, N_EXPERTS, D_MODEL, D_EXPERT), D_MODEL ** -0.5),
        'w_down': nrm(ks[25], (DEPTH, N_EXPERTS, D_EXPERT, D_MODEL), D_EXPERT ** -0.5),
    }


def reference(x_prompt, x_sample, cache_k, cache_v, state_conv, page_table, meta_tokens,
              g_attn, w_in, g_q, g_k, sb_bias, w_dw, b_dw, g_conv_ln, b_conv_ln, g_out_a, g_out_b,
              w_out, g_ffn, w_group_router, b_group_router, w_expert_router, b_expert_router,
              w_gate, w_up, w_down):
    bsz = x_prompt.shape[0]
    db = x_sample.shape[0]
    past = page_table.shape[1] * PAGE_SIZE
    meta = jnp.broadcast_to(meta_tokens.astype(x_prompt.dtype)[None], (bsz, N_META, D_MODEL))
    xp = jnp.concatenate([meta, x_prompt], axis=1)
    xs = x_sample
    kp_l, vp_l, cp_l, ks_l, vs_l, cs_l = [], [], [], [], [], []
    for l in range(DEPTH):
        q, k, v, u = _in_proj(xp, g_attn[l], w_in[l], g_q[l], g_k[l])
        a = _sb_prompt(q, k, v, sb_bias[l])
        c = _conv_module(jnp.pad(u, ((0, 0), (CONV_W - 1, 0), (0, 0))), w_dw[l], b_dw[l], g_conv_ln[l], b_conv_ln[l])
        xp = xp + _out_proj(a, c, g_out_a[l], g_out_b[l], w_out[l])
        xp = xp + _hier_moe(xp, g_ffn[l], w_group_router[l], b_group_router[l], w_expert_router[l],
                            b_expert_router[l], w_gate[l], w_up[l], w_down[l])
        kp_l.append(k)
        vp_l.append(v)
        cp_l.append(u[:, -(CONV_W - 1):])
        qs, kn, vn, us = _in_proj(xs, g_attn[l], w_in[l], g_q[l], g_k[l])
        k_past = cache_k[l][page_table].reshape(db, past, N_HEADS_A, HEAD_DIM)
        v_past = cache_v[l][page_table].reshape(db, past, N_HEADS_A, HEAD_DIM)
        a_s = _sb_sample(qs, kn, vn, k_past, v_past, sb_bias[l])
        u_ext = jnp.concatenate([state_conv[l].astype(us.dtype), us], axis=1)
        c_s = _conv_module(u_ext, w_dw[l], b_dw[l], g_conv_ln[l], b_conv_ln[l])
        xs = xs + _out_proj(a_s, c_s, g_out_a[l], g_out_b[l], w_out[l])
        xs = xs + _hier_moe(xs, g_ffn[l], w_group_router[l], b_group_router[l], w_expert_router[l],
                            b_expert_router[l], w_gate[l], w_up[l], w_down[l])
        ks_l.append(kn)
        vs_l.append(vn)
        cs_l.append(u_ext[:, -(CONV_W - 1):])
    y_prompt = xp[:, N_META:]
    y_sample = xs
    k_prompt = jnp.stack(kp_l)
    v_prompt = jnp.stack(vp_l)
    conv_prompt = jnp.stack(cp_l)
    k_sample = jnp.stack(ks_l)
    v_sample = jnp.stack(vs_l)
    conv_sample = jnp.stack(cs_l)
    return (y_prompt, y_sample, k_prompt, v_prompt, conv_prompt, k_sample, v_sample, conv_sample)
```

```python
import functools

import jax
import jax.numpy as jnp
from jax import lax
from jax.experimental import pallas as pl
from jax.experimental.pallas import tpu as pltpu

F32 = jnp.float32
BF16 = jnp.bfloat16

N_META = 16
N_HEADS = 8
HEAD_DIM = 128
C_A = N_HEADS * HEAD_DIM
C_B = 1024
CONV_W = 31
N_GROUPS = 4
EXPERTS_PER_GROUP = 4
N_EXPERTS = N_GROUPS * EXPERTS_PER_GROUP
TOP_K = 2
PAGE_SIZE = 128
EPS = 1e-6
SB_SCALE = HEAD_DIM ** -0.5

LANES = 128
BLK = 128
TM_IN = 512
TM_OUT = 256
TM_EXP = 256
CONV_HALO = 32
VMEM_LIMIT = 56 * 1024 * 1024


def _cparams(sem):
    return pltpu.CompilerParams(dimension_semantics=sem, vmem_limit_bytes=VMEM_LIMIT)


def _rms(x, g):
    return x * lax.rsqrt(jnp.mean(x * x, axis=-1, keepdims=True) + EPS) * g


def _in_proj_kernel(x_ref, g_ref, w_ref, gq_ref, gk_ref,
                    q_ref, k_ref, v_ref, u_ref, kb_ref, vb_ref, h_scr, ua_scr):
    j = pl.program_id(1)

    @pl.when(j == 0)
    def _():
        h_scr[...] = _rms(x_ref[...], g_ref[...]).astype(BF16)

    p = jnp.dot(h_scr[...], w_ref[...], preferred_element_type=F32)

    def head_norm(g):
        outs = []
        for h in range(N_HEADS):
            ph = p[:, h * HEAD_DIM:(h + 1) * HEAD_DIM]
            outs.append(_rms(ph, g))
        return outs

    @pl.when(j == 0)
    def _():
        for h, qh in enumerate(head_norm(gq_ref[...])):
            q_ref[:, h * HEAD_DIM:(h + 1) * HEAD_DIM] = qh.astype(BF16)

    @pl.when(j == 1)
    def _():
        for h, kh in enumerate(head_norm(gk_ref[...])):
            k_ref[:, h * HEAD_DIM:(h + 1) * HEAD_DIM] = kh
            kb_ref[:, h * HEAD_DIM:(h + 1) * HEAD_DIM] = kh.astype(BF16)

    @pl.when(j == 2)
    def _():
        v_ref[...] = p
        vb_ref[...] = p.astype(BF16)

    @pl.when(j == 3)
    def _():
        ua_scr[...] = p

    @pl.when(j == 4)
    def _():
        u_ref[...] = ua_scr[...] * jax.nn.sigmoid(p)


def _in_proj(x, g, w_bf, gq, gk):
    n, d = x.shape
    seg = C_A
    row = lambda i, j: (i, 0)
    out_f32 = jax.ShapeDtypeStruct((n, seg), F32)
    out_bf = jax.ShapeDtypeStruct((n, seg), BF16)
    return pl.pallas_call(
        _in_proj_kernel,
        grid=(n // TM_IN, 5),
        in_specs=[
            pl.BlockSpec((TM_IN, d), row),
            pl.BlockSpec((1, d), lambda i, j: (0, 0)),
            pl.BlockSpec((d, seg), lambda i, j: (0, j)),
            pl.BlockSpec((1, HEAD_DIM), lambda i, j: (0, 0)),
            pl.BlockSpec((1, HEAD_DIM), lambda i, j: (0, 0)),
        ],
        out_specs=[pl.BlockSpec((TM_IN, seg), row)] * 6,
        out_shape=[out_bf, out_f32, out_f32, out_f32, out_bf, out_bf],
        scratch_shapes=[pltpu.VMEM((TM_IN, d), BF16), pltpu.VMEM((TM_IN, seg), F32)],
        compiler_params=_cparams(("parallel", "arbitrary")),
        name="in_proj",
    )(x, g.reshape(1, d), w_bf, gq.reshape(1, HEAD_DIM), gk.reshape(1, HEAD_DIM))


def _sb_tile(q, kt, vt, bias, suffix_ones, carry, acc, mask):
    s = lax.dot_general(q, kt, (((1,), (1,)), ((), ())), preferred_element_type=F32)
    z = s * SB_SCALE + bias
    lk = jax.nn.log_sigmoid(-z)
    lkm = lk if mask is None else jnp.where(mask, lk, 0.0)
    hi = lkm.astype(BF16)
    lo = (lkm - hi.astype(F32)).astype(BF16)
    r = (jnp.dot(hi, suffix_ones, preferred_element_type=F32)
         + jnp.dot(lo, suffix_ones, preferred_element_type=F32))
    after = r[:, :BLK]
    total = r[:, BLK:]
    w = jnp.exp(z + lk + after + carry)
    if mask is not None:
        w = jnp.where(mask, w, 0.0)
    acc = acc + jnp.dot(w.astype(BF16), vt, preferred_element_type=F32)
    return carry + total, acc


def _suffix_ones():
    j = lax.broadcasted_iota(jnp.int32, (BLK, 2 * BLK), 0)
    s = lax.broadcasted_iota(jnp.int32, (BLK, 2 * BLK), 1)
    return jnp.where((j > s) | (s >= BLK), 1.0, 0.0).astype(BF16)


def _attn_prompt_kernel(bias_ref, q_ref, k_ref, v_ref, so_ref, o_ref):
    qi = pl.program_id(1)
    row = lax.broadcasted_iota(jnp.int32, (BLK, BLK), 0)
    col = lax.broadcasted_iota(jnp.int32, (BLK, BLK), 1)
    diag_mask = col < row
    suffix_ones = so_ref[...]
    zeros = jnp.zeros((BLK, BLK), F32)
    for h in range(N_HEADS):
        lanes = slice(h * HEAD_DIM, (h + 1) * HEAD_DIM)
        q = q_ref[:, lanes]
        bias = bias_ref[h]

        def tile(kblk, carry, acc, mask):
            start = pl.multiple_of(kblk * BLK, BLK)
            kt = k_ref[pl.ds(start, BLK), lanes]
            vt = v_ref[pl.ds(start, BLK), lanes]
            return _sb_tile(q, kt, vt, bias, suffix_ones, carry, acc, mask)

        carry, acc = tile(qi, zeros, zeros, diag_mask)
        carry, acc = lax.fori_loop(
            0, qi, lambda i, c: tile(qi - 1 - i, c[0], c[1], None), (carry, acc))
        o_ref[:, lanes] = acc.astype(BF16)


def _attn_prompt(q, kb, vb, sb_bias, n_batch, t_pad):
    nq = t_pad // BLK
    return pl.pallas_call(
        _attn_prompt_kernel,
        grid_spec=pltpu.PrefetchScalarGridSpec(
            num_scalar_prefetch=1,
            grid=(n_batch, nq),
            in_specs=[
                pl.BlockSpec((BLK, C_A), lambda b, i, s: (b * nq + i, 0)),
                pl.BlockSpec((t_pad, C_A), lambda b, i, s: (b, 0)),
                pl.BlockSpec((t_pad, C_A), lambda b, i, s: (b, 0)),
                pl.BlockSpec((BLK, 2 * BLK), lambda b, i, s: (0, 0)),
            ],
            out_specs=pl.BlockSpec((BLK, C_A), lambda b, i, s: (b * nq + i, 0)),
        ),
        out_shape=jax.ShapeDtypeStruct((n_batch * t_pad, C_A), BF16),
        compiler_params=_cparams(("parallel", "arbitrary")),
        name="attn_prompt",
    )(sb_bias.astype(F32), q, kb, vb, _suffix_ones())


N_QH = 32


def _attn_sample_kernel(pt_ref, q_ref, kn_ref, vn_ref, kp_ref, vp_ref, bias_ref, so_ref,
                        o_ref, qbd_scr, carry_scr, acc_scr, *, n_pages, n_steps):
    pg = pl.program_id(1)
    suffix_ones = so_ref[...]
    bias = bias_ref[...]
    head_of_row = lax.broadcasted_iota(jnp.int32, (N_HEADS, C_A), 0)
    head_of_lane = lax.broadcasted_iota(jnp.int32, (N_HEADS, C_A), 1) // HEAD_DIM
    head_mask = head_of_row == head_of_lane

    def heads_to_lanes(ref):
        return jnp.concatenate(
            [ref[pl.ds(h, PAGE_SIZE, stride=N_HEADS), :] for h in range(N_HEADS)], axis=1)

    @pl.when(pg == 0)
    def _():
        qs = q_ref[0]
        rows = [jnp.where(head_mask, jnp.broadcast_to(qs[i:i + 1, :], (N_HEADS, C_A)), 0.0)
                for i in range(n_steps)]
        qbd_scr[...] = jnp.concatenate(rows, axis=0).astype(BF16)
        pad = jnp.zeros((BLK - kn_ref.shape[1], C_A), F32)
        kt = jnp.concatenate([kn_ref[0], pad], axis=0).astype(BF16)
        vt = jnp.concatenate([vn_ref[0], pad], axis=0).astype(BF16)
        step_of_row = lax.broadcasted_iota(jnp.int32, (N_QH, BLK), 0) // N_HEADS
        key = lax.broadcasted_iota(jnp.int32, (N_QH, BLK), 1)
        carry, acc = _sb_tile(qbd_scr[...], kt, vt, bias, suffix_ones,
                              jnp.zeros((N_QH, BLK), F32), jnp.zeros((N_QH, C_A), F32),
                              key < step_of_row)
        carry_scr[...] = carry
        acc_scr[...] = acc

    kt = heads_to_lanes(kp_ref).astype(BF16)
    vt = heads_to_lanes(vp_ref).astype(BF16)
    carry, acc = _sb_tile(qbd_scr[...], kt, vt, bias, suffix_ones,
                          carry_scr[...], acc_scr[...], None)
    carry_scr[...] = carry
    acc_scr[...] = acc

    @pl.when(pg == n_pages - 1)
    def _():
        for i in range(n_steps):
            blk = acc[i * N_HEADS:(i + 1) * N_HEADS, :]
            o_ref[0, i:i + 1, :] = jnp.sum(jnp.where(head_mask, blk, 0.0), axis=0, keepdims=True)


def _attn_sample(q_s, k_new8, v_new8, cache_k, cache_v, layer, page_table, sb_bias):
    n_seq, n_steps, _ = q_s.shape
    n_pages = page_table.shape[1]
    depth, n_pool = cache_k.shape[:2]
    rows = PAGE_SIZE * N_HEADS
    ck = cache_k.reshape(depth, n_pool, rows, HEAD_DIM)
    cv = cache_v.reshape(depth, n_pool, rows, HEAD_DIM)
    bias_rows = jnp.broadcast_to(jnp.tile(sb_bias.astype(F32), n_steps)[:, None], (N_QH, BLK))
    page = lambda b, p, pt: (layer, pt[b * n_pages + (n_pages - 1 - p)], 0, 0)
    seq = lambda b, p, pt: (b, 0, 0)
    return pl.pallas_call(
        functools.partial(_attn_sample_kernel, n_pages=n_pages, n_steps=n_steps),
        grid_spec=pltpu.PrefetchScalarGridSpec(
            num_scalar_prefetch=1,
            grid=(n_seq, n_pages),
            in_specs=[
                pl.BlockSpec((1, n_steps, C_A), seq),
                pl.BlockSpec((1, 8, C_A), seq),
                pl.BlockSpec((1, 8, C_A), seq),
                pl.BlockSpec((None, None, rows, HEAD_DIM), page),
                pl.BlockSpec((None, None, rows, HEAD_DIM), page),
                pl.BlockSpec((N_QH, BLK), lambda b, p, pt: (0, 0)),
                pl.BlockSpec((BLK, 2 * BLK), lambda b, p, pt: (0, 0)),
            ],
            out_specs=pl.BlockSpec((1, n_steps, C_A), seq),
            scratch_shapes=[pltpu.VMEM((N_QH, C_A), BF16),
                            pltpu.VMEM((N_QH, BLK), F32),
                            pltpu.VMEM((N_QH, C_A), F32)],
        ),
        out_shape=jax.ShapeDtypeStruct((n_seq, n_steps, C_A), F32),
        compiler_params=_cparams(("parallel", "arbitrary")),
        name="attn_sample",
    )(page_table.reshape(-1), q_s, k_new8, v_new8, ck, cv, bias_rows, _suffix_ones())


CONV_CT = 256


def _conv_prompt_kernel(u_ref, w_ref, y_ref, ext_scr, *, t_pad):
    ext_scr[0:CONV_HALO, :] = jnp.zeros((CONV_HALO, CONV_CT), F32)
    ext_scr[CONV_HALO:, :] = u_ref[...]
    w = w_ref[...]
    off = CONV_HALO - (CONV_W - 1)
    for t0 in range(0, t_pad, BLK):
        acc = jnp.zeros((BLK, CONV_CT), F32)
        for j in range(CONV_W):
            acc = acc + w[j:j + 1, :] * ext_scr[t0 + off + j:t0 + off + j + BLK, :]
        y_ref[t0:t0 + BLK, :] = acc


def _conv_prompt(u, w_dw, n_batch, t_pad):
    return pl.pallas_call(
        functools.partial(_conv_prompt_kernel, t_pad=t_pad),
        grid=(n_batch, C_B // CONV_CT),
        in_specs=[pl.BlockSpec((t_pad, CONV_CT), lambda b, c: (b, c)),
                  pl.BlockSpec((CONV_W, CONV_CT), lambda b, c: (0, c))],
        out_specs=pl.BlockSpec((t_pad, CONV_CT), lambda b, c: (b, c)),
        out_shape=jax.ShapeDtypeStruct((n_batch * t_pad, C_B), F32),
        scratch_shapes=[pltpu.VMEM((CONV_HALO + t_pad, CONV_CT), F32)],
        compiler_params=_cparams(("parallel", "parallel")),
        name="conv_prompt",
    )(u, w_dw)


def _conv_sample_kernel(u_ref, w_ref, y_ref, *, n_steps):
    w = w_ref[...]
    for i in range(n_steps):
        acc = jnp.zeros(u_ref.shape[1:], F32)
        for j in range(CONV_W):
            acc = acc + w[j:j + 1, :] * u_ref[i + j]
        y_ref[i] = acc


def _conv_sample(u_ext_t, w_dw, n_steps):
    t_ext, n_seq, _ = u_ext_t.shape
    return pl.pallas_call(
        functools.partial(_conv_sample_kernel, n_steps=n_steps),
        grid=(C_B // CONV_CT,),
        in_specs=[pl.BlockSpec((t_ext, n_seq, CONV_CT), lambda c: (0, 0, c)),
                  pl.BlockSpec((CONV_W, CONV_CT), lambda c: (0, c))],
        out_specs=pl.BlockSpec((n_steps, n_seq, CONV_CT), lambda c: (0, 0, c)),
        out_shape=jax.ShapeDtypeStruct((n_steps, n_seq, C_B), F32),
        compiler_params=_cparams(("parallel",)),
        name="conv_sample",
    )(u_ext_t, w_dw)


ROUTE_W = LANES


def _dot_f32(a, b_hi, b_lo):
    a_hi = a.astype(BF16)
    a_lo = (a - a_hi.astype(F32)).astype(BF16)
    return (jnp.dot(a_hi, b_hi, preferred_element_type=F32)
            + jnp.dot(a_lo, b_hi, preferred_element_type=F32)
            + jnp.dot(a_hi, b_lo, preferred_element_type=F32))


def _out_proj_kernel(a_ref, y_ref, x_ref, goa_ref, bdw_ref, gln_ref, bln_ref, gob_ref, wo_ref,
                     gffn_ref, wr_hi_ref, wr_lo_ref, br_ref, x1_ref, h_ref, route_ref):
    an = _rms(a_ref[...].astype(F32), goa_ref[...])
    y = y_ref[...] + bdw_ref[...]
    mu = jnp.mean(y, axis=-1, keepdims=True)
    yc = y - mu
    var = jnp.mean(yc * yc, axis=-1, keepdims=True)
    c = jax.nn.silu(yc * lax.rsqrt(var + EPS) * gln_ref[...] + bln_ref[...])
    cn = _rms(c, gob_ref[...])
    m = (jnp.dot(an.astype(BF16), wo_ref[0:C_A, :], preferred_element_type=F32)
         + jnp.dot(cn.astype(BF16), wo_ref[C_A:, :], preferred_element_type=F32))
    x1 = x_ref[...] + m
    x1_ref[...] = x1
    h = _rms(x1, gffn_ref[...])
    h_ref[...] = h.astype(BF16)

    logits = _dot_f32(h, wr_hi_ref[...], wr_lo_ref[...]) + br_ref[...]
    lane_i = lax.broadcasted_iota(jnp.int32, logits.shape, 1)
    lane = lane_i.astype(F32)
    group_of_lane = ((lane_i - N_GROUPS) // EXPERTS_PER_GROUP).astype(F32)
    neg = jnp.float32(-jnp.inf)

    def top1(vals, valid):
        v = jnp.max(jnp.where(valid, vals, neg), axis=-1, keepdims=True)
        idx = jnp.min(jnp.where(valid & (vals == v), lane, float(ROUTE_W)), axis=-1, keepdims=True)
        return v, idx

    is_group = lane_i < N_GROUPS
    g_max, g_sel = top1(logits, is_group)
    p_g = 1.0 / jnp.sum(jnp.where(is_group, jnp.exp(logits - g_max), 0.0), axis=-1, keepdims=True)
    in_group = ((lane_i >= N_GROUPS) & (lane_i < N_GROUPS + N_EXPERTS) & (group_of_lane == g_sel))
    v1, i1 = top1(logits, in_group)
    v2, i2 = top1(logits, in_group & (lane != i1))
    e21 = jnp.exp(v2 - v1)
    w1 = p_g / (1.0 + e21)
    w2 = p_g * e21 / (1.0 + e21)
    route = jnp.where(lane_i == 0, i1 - N_GROUPS,
                      jnp.where(lane_i == 1, i2 - N_GROUPS,
                                jnp.where(lane_i == 2, w1, jnp.where(lane_i == 3, w2, 0.0))))
    route_ref[...] = route


def _out_proj(a, y, x, goa, bdw, gln, bln, gob, wo_bf, gffn, wr_hi, wr_lo, br):
    n, d = x.shape
    row = lambda i: (i, 0)
    fix = lambda i: (0, 0)
    vec = lambda v: v.reshape(1, -1).astype(F32)
    return pl.pallas_call(
        _out_proj_kernel,
        grid=(n // TM_OUT,),
        in_specs=[
            pl.BlockSpec((TM_OUT, C_A), row),
            pl.BlockSpec((TM_OUT, C_B), row),
            pl.BlockSpec((TM_OUT, d), row),
            pl.BlockSpec((1, C_A), fix),
            pl.BlockSpec((1, C_B), fix),
            pl.BlockSpec((1, C_B), fix),
            pl.BlockSpec((1, C_B), fix),
            pl.BlockSpec((1, C_B), fix),
            pl.BlockSpec((C_A + C_B, d), fix),
            pl.BlockSpec((1, d), fix),
            pl.BlockSpec((d, ROUTE_W), fix),
            pl.BlockSpec((d, ROUTE_W), fix),
            pl.BlockSpec((1, ROUTE_W), fix),
        ],
        out_specs=[pl.BlockSpec((TM_OUT, d), row), pl.BlockSpec((TM_OUT, d), row),
                   pl.BlockSpec((TM_OUT, ROUTE_W), row)],
        out_shape=[jax.ShapeDtypeStruct((n, d), F32), jax.ShapeDtypeStruct((n, d), BF16),
                   jax.ShapeDtypeStruct((n, ROUTE_W), F32)],
        compiler_params=_cparams(("parallel",)),
        name="out_proj_router",
    )(a, y, x, vec(goa), vec(bdw), vec(gln), vec(bln), vec(gob), wo_bf, vec(gffn),
      wr_hi, wr_lo, vec(br))


def _experts_kernel(te_ref, nt_ref, xs_ref, ws_ref, wg_ref, wu_ref, wd_ref, ys_ref):
    i = pl.program_id(0)

    @pl.when(i < nt_ref[0])
    def _():
        xs = xs_ref[...]
        hg = jnp.dot(xs, wg_ref[...], preferred_element_type=F32)
        hu = jnp.dot(xs, wu_ref[...], preferred_element_type=F32)
        act = jax.nn.silu(hg) * hu * ws_ref[...]
        ys_ref[...] = jnp.dot(act.astype(BF16), wd_ref[...], preferred_element_type=F32)

    @pl.when(i >= nt_ref[0])
    def _():
        ys_ref[...] = jnp.zeros(ys_ref.shape, F32)


def _experts(tile_expert, n_tiles_used, xs, ws, wg_bf, wu_bf, wd_bf):
    n_rows, d = xs.shape
    f = wg_bf.shape[-1]
    return pl.pallas_call(
        _experts_kernel,
        grid_spec=pltpu.PrefetchScalarGridSpec(
            num_scalar_prefetch=2,
            grid=(n_rows // TM_EXP,),
            in_specs=[
                pl.BlockSpec((TM_EXP, d), lambda i, te, nt: (i, 0)),
                pl.BlockSpec((TM_EXP, 1), lambda i, te, nt: (i, 0)),
                pl.BlockSpec((None, d, f), lambda i, te, nt: (te[i], 0, 0)),
                pl.BlockSpec((None, d, f), lambda i, te, nt: (te[i], 0, 0)),
                pl.BlockSpec((None, f, d), lambda i, te, nt: (te[i], 0, 0)),
            ],
            out_specs=pl.BlockSpec((TM_EXP, d), lambda i, te, nt: (i, 0)),
        ),
        out_shape=jax.ShapeDtypeStruct((n_rows, d), F32),
        compiler_params=_cparams(("arbitrary",)),
        name="experts",
    )(tile_expert, n_tiles_used, xs, ws, wg_bf, wu_bf, wd_bf)


def _routing_plan(route, n_rows):
    e = jnp.concatenate([route[:, 0], route[:, 1]]).astype(jnp.int32)
    w = jnp.concatenate([route[:, 2], route[:, 3]])
    tok = jnp.tile(jnp.arange(n_rows, dtype=jnp.int32), TOP_K)
    onehot = (e[:, None] == jnp.arange(N_EXPERTS, dtype=jnp.int32)[None, :]).astype(jnp.int32)
    rank = jnp.sum((jnp.cumsum(onehot, axis=0) - onehot) * onehot, axis=1)
    counts = jnp.sum(onehot, axis=0)
    tiles_per_e = (counts + TM_EXP - 1) // TM_EXP
    tile_end = jnp.cumsum(tiles_per_e)
    row_start = (tile_end - tiles_per_e) * TM_EXP
    pos = row_start[e] + rank
    n_tiles = (TOP_K * n_rows) // TM_EXP + N_EXPERTS
    n_sorted = n_tiles * TM_EXP
    src_tok = jnp.zeros((n_sorted,), jnp.int32).at[pos].set(tok)
    ws = jnp.zeros((n_sorted,), F32).at[pos].set(w)
    tile_expert = jnp.minimum(
        jnp.searchsorted(tile_end, jnp.arange(n_tiles, dtype=jnp.int32), side="right"),
        N_EXPERTS - 1).astype(jnp.int32)
    n_used = tile_end[-1:].astype(jnp.int32)
    return src_tok, ws.reshape(-1, 1), tile_expert, n_used, pos


def kernel(x_prompt, x_sample, cache_k, cache_v, state_conv, page_table, meta_tokens, g_attn, w_in, g_q, g_k, sb_bias, w_dw, b_dw, g_conv_ln, b_conv_ln, g_out_a, g_out_b, w_out, g_ffn, w_group_router, b_group_router, w_expert_router, b_expert_router, w_gate, w_up, w_down):
    n_batch, seq, d = x_prompt.shape
    n_seq, n_steps, _ = x_sample.shape
    depth = w_in.shape[0]
    t_real = N_META + seq
    t_pad = -(-t_real // BLK) * BLK
    n_prompt_rows = n_batch * t_pad
    n_sample_rows = n_seq * n_steps
    n_rows = n_prompt_rows + n_sample_rows
    assert n_rows % TM_IN == 0 and n_rows % TM_OUT == 0 and (TOP_K * n_rows) % TM_EXP == 0

    meta = jnp.broadcast_to(meta_tokens.astype(F32)[None], (n_batch, N_META, d))
    xp = jnp.concatenate([meta, x_prompt, jnp.zeros((n_batch, t_pad - t_real, d), F32)], axis=1)
    x = jnp.concatenate([xp.reshape(n_prompt_rows, d), x_sample.reshape(n_sample_rows, d)], axis=0)

    kp_l, vp_l, cp_l, ks_l, vs_l, cs_l = [], [], [], [], [], []
    for l in range(depth):
        q, k, v, u, kb, vb = _in_proj(x, g_attn[l], w_in[l].astype(BF16), g_q[l], g_k[l])

        a_p = _attn_prompt(q, kb, vb, sb_bias[l], n_batch, t_pad)
        y_p = _conv_prompt(u, w_dw[l], n_batch, t_pad)

        sample = lambda arr: arr[n_prompt_rows:].reshape(n_seq, n_steps, -1)
        k_s, v_s, u_s = sample(k), sample(v), sample(u)
        pad8 = lambda arr: jnp.pad(arr, ((0, 0), (0, 8 - n_steps), (0, 0)))
        a_s = _attn_sample(sample(q).astype(F32), pad8(k_s), pad8(v_s), cache_k, cache_v, l, page_table,
                           sb_bias[l])
        u_ext = jnp.concatenate([state_conv[l].astype(F32), u_s], axis=1)
        y_s = _conv_sample(jnp.transpose(u_ext, (1, 0, 2)), w_dw[l], n_steps)
        y_s = jnp.transpose(y_s, (1, 0, 2)).reshape(n_sample_rows, C_B)

        a = jnp.concatenate([a_p, a_s.reshape(n_sample_rows, C_A).astype(BF16)], axis=0)
        y = jnp.concatenate([y_p, y_s], axis=0)

        w_r = jnp.concatenate([w_group_router[l], w_expert_router[l]], axis=1)
        w_r = jnp.pad(w_r, ((0, 0), (0, ROUTE_W - w_r.shape[1])))
        w_r_hi = w_r.astype(BF16)
        w_r_lo = (w_r - w_r_hi.astype(F32)).astype(BF16)
        b_r = jnp.concatenate([b_group_router[l], b_expert_router[l]])
        b_r = jnp.pad(b_r, (0, ROUTE_W - b_r.shape[0]))
        x1, h, route = _out_proj(a, y, x, g_out_a[l], b_dw[l], g_conv_ln[l], b_conv_ln[l],
                                 g_out_b[l], w_out[l].astype(BF16), g_ffn[l], w_r_hi, w_r_lo, b_r)

        src_tok, ws, tile_expert, n_used, pos = _routing_plan(route, n_rows)
        xs = jnp.take(h, src_tok, axis=0)
        ys = _experts(tile_expert, n_used, xs, ws, w_gate[l].astype(BF16), w_up[l].astype(BF16),
                      w_down[l].astype(BF16))
        x = x1 + jnp.take(ys, pos[:n_rows], axis=0) + jnp.take(ys, pos[n_rows:], axis=0)

        prompt = lambda arr: arr[:n_prompt_rows].reshape(n_batch, t_pad, -1)[:, :t_real]
        kp_l.append(prompt(k).reshape(n_batch, t_real, N_HEADS, HEAD_DIM))
        vp_l.append(prompt(v).reshape(n_batch, t_real, N_HEADS, HEAD_DIM))
        cp_l.append(prompt(u)[:, -(CONV_W - 1):])
        ks_l.append(k_s.reshape(n_seq, n_steps, N_HEADS, HEAD_DIM))
        vs_l.append(v_s.reshape(n_seq, n_steps, N_HEADS, HEAD_DIM))
        cs_l.append(u_ext[:, -(CONV_W - 1):])

    y_prompt = x[:n_prompt_rows].reshape(n_batch, t_pad, d)[:, N_META:t_real]
    y_sample = x[n_prompt_rows:].reshape(n_seq, n_steps, d)
    return (y_prompt, y_sample, jnp.stack(kp_l), jnp.stack(vp_l), jnp.stack(cp_l),
            jnp.stack(ks_l), jnp.stack(vs_l), jnp.stack(cs_l))
```
